```python
import jax, jax.numpy as jnp
from jax import lax
import numpy as np

D_MODEL = 2048
BATCH = 16
SEQ = 256
DEPTH = 2
DEC_BATCH = 8
DEC_SEQ = 4096
PAST_LEN = 256

GRID_W = 64
W_A = D_MODEL // 2
W_B = D_MODEL - W_A
HA_HEADS = 8
HA_DK = W_A // HA_HEADS
HA_DV = W_A // HA_HEADS
HGRN_CHUNK = 64
HB_GROUPS = 8
HB_DIM = W_B // HB_GROUPS
MLP_CHUNK = 128
D_FF = 5632
N_EXPERTS = 8
TOP_K = 2
D_FF_E = 7168
MOE_BLOCK = 256
N_DENSE = (DEPTH + 1) // 2
N_MOE = DEPTH // 2
D_IN = 5 * W_A + 2 * W_B
N_MOD = 6
EPS = 1e-6

kernel_name = "hybrid_hgrn2_gmlp_diffusion_step"


def _rmsnorm(x, w):
    xf = x.astype(jnp.float32)
    y = xf * lax.rsqrt(jnp.mean(xf * xf, axis=-1, keepdims=True) + EPS)
    return (y * w.astype(jnp.float32)).astype(x.dtype)


def _gla_scan(q, k, v, logf, s0):
    B, L, H, _ = q.shape
    n = L // HGRN_CHUNK

    def to_chunks(t):
        t = t.astype(jnp.float32).reshape(B, n, HGRN_CHUNK, H, t.shape[-1])
        return jnp.moveaxis(t, 1, 0)

    mask = jnp.tril(jnp.ones((HGRN_CHUNK, HGRN_CHUNK), dtype=bool))[None, :, :, None, None]

    def step(S, xs):
        qc, kc, vc, gc = xs
        b = jnp.cumsum(gc, axis=1)
        o_inter = jnp.einsum('bthk,bhkv->bthv', qc * jnp.exp(b), S)
        diff = b[:, :, None] - b[:, None, :]
        decay = jnp.exp(jnp.where(mask, diff, -jnp.inf))
        att = jnp.einsum('bthk,bjhk,btjhk->bthj', qc, kc, decay)
        o_intra = jnp.einsum('bthj,bjhv->bthv', att, vc)
        b_last = b[:, -1]
        S_new = jnp.exp(b_last)[..., None] * S + jnp.einsum(
            'bjhk,bjhv->bhkv', kc * jnp.exp(b_last[:, None] - b), vc)
        return S_new, o_inter + o_intra

    S_fin, o = lax.scan(step, s0.astype(jnp.float32),
                        (to_chunks(q), to_chunks(k), to_chunks(v), to_chunks(logf)))
    o = jnp.moveaxis(o, 0, 1).reshape(B, L, H, v.shape[-1])
    return o, S_fin


def _hgrn2(h_q, h_i, h_ff, h_fb, h_g, lower, s0, w_onorm):
    B, L, _ = h_q.shape
    q = jax.nn.silu(h_q).reshape(B, L, HA_HEADS, HA_DK)
    v = h_i.reshape(B, L, HA_HEADS, HA_DV)
    if s0 is None:
        s0 = jnp.zeros((B, 2, HA_HEADS, HA_DK, HA_DV), jnp.float32)
    outs, finals = [], []
    for d, (hf, rev) in enumerate(((h_ff, False), (h_fb, True))):
        f = lower[d] + (1.0 - lower[d]) * jax.nn.sigmoid(hf.astype(jnp.float32))
        k = (1.0 - f).reshape(B, L, HA_HEADS, HA_DK)
        logf = jnp.log(f).reshape(B, L, HA_HEADS, HA_DK)
        qd, kd, vd, gd = (jnp.flip(t, axis=1) if rev else t for t in (q, k, v, logf))
        o, s = _gla_scan(qd, kd, vd, gd, s0[:, d])
        outs.append(jnp.flip(o, axis=1) if rev else o)
        finals.append(s)
    o = (outs[0] + outs[1]).reshape(B, L, W_A).astype(h_q.dtype)
    o = _rmsnorm(o, w_onorm) * jax.nn.silu(h_g)
    return o, jnp.stack(finals, axis=1)


def _chunk_mlp(u, v, n_chunks, w_vnorm, w_s, b_s):
    B = u.shape[0]
    vn = _rmsnorm(v, w_vnorm).reshape(B, n_chunks, MLP_CHUNK, HB_GROUPS, HB_DIM)
    s = jnp.einsum('gts,bnsgd->bntgd', w_s, vn) + b_s.T[None, None, :, :, None]
    return u * s.reshape(u.shape)


def _swiglu(x, w_gate, w_up, w_down):
    return (jax.nn.silu(x @ w_gate) * (x @ w_up)) @ w_down


def _moe_swiglu(x, w_router, w_gate, w_up, w_down):
    B, L, D = x.shape
    xt = x.reshape(-1, D)
    N = xt.shape[0]
    logits = (xt @ w_router).astype(jnp.float32)
    top_v, top_i = lax.top_k(logits, TOP_K)
    top_w = jax.nn.softmax(top_v, axis=-1).astype(x.dtype)
    A = N * TOP_K
    nb = (A + N_EXPERTS * (MOE_BLOCK - 1) + MOE_BLOCK - 1) // MOE_BLOCK
    e_flat = top_i.reshape(-1).astype(jnp.int32)
    t_flat = jnp.repeat(jnp.arange(N, dtype=jnp.int32), TOP_K)
    w_flat = top_w.reshape(-1)
    order = jnp.argsort(e_flat)
    se = e_flat[order]
    counts = jnp.bincount(e_flat, length=N_EXPERTS).astype(jnp.int32)
    starts = jnp.cumsum(counts) - counts
    pcounts = (counts + MOE_BLOCK - 1) // MOE_BLOCK * MOE_BLOCK
    pends = jnp.cumsum(pcounts)
    pstarts = pends - pcounts
    dest = pstarts[se] + jnp.arange(A, dtype=jnp.int32) - starts[se]
    tok_buf = jnp.zeros((nb * MOE_BLOCK,), jnp.int32).at[dest].set(t_flat[order])
    w_buf = jnp.zeros((nb * MOE_BLOCK,), x.dtype).at[dest].set(w_flat[order])
    blk_e = jnp.minimum(
        jnp.searchsorted(pends, jnp.arange(nb, dtype=jnp.int32) * MOE_BLOCK, side='right'),
        N_EXPERTS - 1)

    def expert_block(args):
        tok, e = args
        return _swiglu(xt[tok], w_gate[e], w_up[e], w_down[e])

    yb = lax.map(expert_block, (tok_buf.reshape(nb, MOE_BLOCK), blk_e))
    y = jnp.zeros_like(xt).at[tok_buf].add(yb.reshape(-1, D) * w_buf[:, None])
    return y.reshape(B, L, D)


def _layer(x, mod, s0, n_chunks, lower, l, P):
    sh1, sc1, g1, sh2, sc2, g2 = jnp.split(mod[:, None, :], N_MOD, axis=-1)
    h = _rmsnorm(x, P['pre_mix_norm'][l]) * (1.0 + sc1) + sh1
    proj = h @ P['w_in'][l]
    h_q, h_i, h_ff, h_fb, h_g, h_u, h_v = jnp.split(
        proj, [W_A, 2 * W_A, 3 * W_A, 4 * W_A, 5 * W_A, 5 * W_A + W_B], axis=-1)
    o_a, s_fin = _hgrn2(h_q, h_i, h_ff, h_fb, h_g, lower, s0, P['hgrn_out_norm'][l])
    o_b = _chunk_mlp(h_u, h_v, n_chunks, P['gmlp_v_norm'][l], P['w_spatial'][l], P['b_spatial'][l])
    y = jnp.concatenate([o_a, o_b], axis=-1) @ P['w_out'][l]
    x = x + g1 * _rmsnorm(y, P['post_mix_norm'][l])
    h = _rmsnorm(x, P['pre_ffn_norm'][l]) * (1.0 + sc2) + sh2
    if l % 2 == 0:
        i = l // 2
        f = _swiglu(h, P['ffn_w_gate'][i], P['ffn_w_up'][i], P['ffn_w_down'][i])
    else:
        i = l // 2
        f = _moe_swiglu(h, P['moe_router'][i], P['moe_w_gate'][i], P['moe_w_up'][i], P['moe_w_down'][i])
    x = x + g2 * _rmsnorm(f, P['post_ffn_norm'][l])
    return x, s_fin


def setup_inputs(seed: int = 0) -> dict:
    key = jax.random.key(seed)
    ks = jax.random.split(key, 32)
    nrm = jax.random.normal
    f32 = jnp.float32

    def gain(k, shape):
        return 1.0 + 0.02 * nrm(k, shape, f32)

    return {
        "x_prompt": nrm(ks[0], (BATCH, SEQ, D_MODEL), f32),
        "x_sample": nrm(ks[1], (DEC_BATCH, DEC_SEQ, D_MODEL), f32),
        "c": nrm(ks[2], (DEC_BATCH, D_MODEL), f32),
        "state_hgrn": 0.5 * nrm(ks[3], (DEC_BATCH, DEPTH, 2, HA_HEADS, HA_DK, HA_DV), f32),
        "c_ctx": nrm(ks[4], (D_MODEL,), f32),
        "w_mod": nrm(ks[5], (DEPTH, D_MODEL, N_MOD * D_MODEL), f32) * D_MODEL ** -0.5,
        "b_mod": 0.02 * nrm(ks[6], (DEPTH, N_MOD * D_MODEL), f32),
        "pre_mix_norm": gain(ks[7], (DEPTH, D_MODEL)),
        "w_in": nrm(ks[8], (DEPTH, D_MODEL, D_IN), f32) * D_MODEL ** -0.5,
        "lb_logits": 0.5 * nrm(ks[9], (DEPTH, 2, W_A), f32),
        "hgrn_out_norm": gain(ks[10], (DEPTH, W_A)),
        "gmlp_v_norm": gain(ks[11], (DEPTH, W_B)),
        "w_spatial": nrm(ks[12], (DEPTH, HB_GROUPS, MLP_CHUNK, MLP_CHUNK), f32) * MLP_CHUNK ** -0.5,
        "b_spatial": 1.0 + 0.1 * nrm(ks[13], (DEPTH, HB_GROUPS, MLP_CHUNK), f32),
        "w_out": nrm(ks[14], (DEPTH, W_A + W_B, D_MODEL), f32) * (W_A + W_B) ** -0.5,
        "post_mix_norm": gain(ks[15], (DEPTH, D_MODEL)),
        "pre_ffn_norm": gain(ks[16], (DEPTH, D_MODEL)),
        "ffn_w_gate": nrm(ks[17], (N_DENSE, D_MODEL, D_FF), f32) * D_MODEL ** -0.5,
        "ffn_w_up": nrm(ks[18], (N_DENSE, D_MODEL, D_FF), f32) * D_MODEL ** -0.5,
        "ffn_w_down": nrm(ks[19], (N_DENSE, D_FF, D_MODEL), f32) * D_FF ** -0.5,
        "post_ffn_norm": gain(ks[20], (DEPTH, D_MODEL)),
        "moe_router": nrm(ks[21], (N_MOE, D_MODEL, N_EXPERTS), f32) * D_MODEL ** -0.5,
        "moe_w_gate": nrm(ks[22], (N_MOE, N_EXPERTS, D_MODEL, D_FF_E), f32) * D_MODEL ** -0.5,
        "moe_w_up": nrm(ks[23], (N_MOE, N_EXPERTS, D_MODEL, D_FF_E), f32) * D_MODEL ** -0.5,
        "moe_w_down": nrm(ks[24], (N_MOE, N_EXPERTS, D_FF_E, D_MODEL), f32) * D_FF_E ** -0.5,
    }


def reference(x_prompt, x_sample, c, state_hgrn, c_ctx, w_mod, b_mod, pre_mix_norm, w_in,
              lb_logits, hgrn_out_norm, gmlp_v_norm, w_spatial, b_spatial, w_out, post_mix_norm,
              pre_ffn_norm, ffn_w_gate, ffn_w_up, ffn_w_down, post_ffn_norm, moe_router,
              moe_w_gate, moe_w_up, moe_w_down):
    P = dict(pre_mix_norm=pre_mix_norm, w_in=w_in, hgrn_out_norm=hgrn_out_norm,
             gmlp_v_norm=gmlp_v_norm, w_spatial=w_spatial, b_spatial=b_spatial, w_out=w_out,
             post_mix_norm=post_mix_norm, pre_ffn_norm=pre_ffn_norm, ffn_w_gate=ffn_w_gate,
             ffn_w_up=ffn_w_up, ffn_w_down=ffn_w_down, post_ffn_norm=post_ffn_norm,
             moe_router=moe_router, moe_w_gate=moe_w_gate, moe_w_up=moe_w_up,
             moe_w_down=moe_w_down)
    lb_soft = jax.nn.softmax(lb_logits.astype(jnp.float32), axis=0)
    lb_cum = jnp.cumsum(lb_soft, axis=0)
    lower = lb_cum - lb_cum[0:1]

    xp = x_prompt
    ctx_states = []
    for l in range(DEPTH):
        mod = jax.nn.silu(c_ctx)[None, :] @ w_mod[l] + b_mod[l]
        xp, s = _layer(xp, mod, None, xp.shape[1] // MLP_CHUNK, lower[l], l, P)
        ctx_states.append(s.astype(x_prompt.dtype))
    new_state_hgrn = jnp.stack(ctx_states, axis=1)

    rows = x_sample.shape[1] // GRID_W
    n_lat_chunks = rows // (MLP_CHUNK // GRID_W)
    xs = x_sample
    for l in range(DEPTH):
        mod = jax.nn.silu(c) @ w_mod[l] + b_mod[l]
        xs, _ = _layer(xs, mod, state_hgrn[:, l], n_lat_chunks, lower[l], l, P)
    return (xp, xs, new_state_hgrn)
```

```python
import functools

import jax
import jax.numpy as jnp
from jax import lax
from jax.experimental import pallas as pl
from jax.experimental.pallas import tpu as pltpu

F32 = jnp.float32
BF16 = jnp.bfloat16

EPS = 1e-6
N_MOD = 6
HA_HEADS = 8
HB_GROUPS = 8
HGRN_CHUNK = 64
HGRN_SUB = 16
MLP_CHUNK = 128
N_EXPERTS_LANES = 128
V7X_VMEM_LIMIT = 56 * 1024 * 1024

TM_PROJ = 1024
TM_MIX = 512
TM_FFN = 512
TM_ROUTE = 512
TM_MOE = 512
TF_FFN = 512
TF_MOE = 512


def _params(*sem):
    return pltpu.CompilerParams(dimension_semantics=sem, vmem_limit_bytes=V7X_VMEM_LIMIT)


def _rms(x, w):
    return x * lax.rsqrt(jnp.mean(x * x, axis=-1, keepdims=True) + EPS) * w


def _silu(x):
    return x * jax.nn.sigmoid(x)


def _dot(a, b):
    return jnp.dot(a, b, preferred_element_type=F32)


def _dot_nt(a, b):
    return lax.dot_general(a, b, (((1,), (1,)), ((), ())), preferred_element_type=F32)


def _mod_kernel(c_ref, w_ref, b_ref, o_ref):
    s = _silu(c_ref[...])
    o_ref[0] = jnp.dot(s, w_ref[0], precision=lax.Precision.HIGHEST,
                       preferred_element_type=F32) + b_ref[0]


def _mod_vectors(cvecs, w_mod, b_mod):
    depth, d, n = w_mod.shape
    r = cvecs.shape[0]
    tn = 1024
    return pl.pallas_call(
        _mod_kernel,
        grid=(depth, n // tn),
        in_specs=[pl.BlockSpec((r, d), lambda l, j: (0, 0)),
                  pl.BlockSpec((1, d, tn), lambda l, j: (l, 0, j)),
                  pl.BlockSpec((1, 1, tn), lambda l, j: (l, 0, j))],
        out_specs=pl.BlockSpec((1, r, tn), lambda l, j: (l, 0, j)),
        out_shape=jax.ShapeDtypeStruct((depth, r, n), F32),
        compiler_params=_params("parallel", "parallel"),
        name="mod_vectors",
    )(cvecs, w_mod, b_mod.reshape(depth, 1, n))


def _mod_spec(which, d, tm, geom):
    rows_ctx, rows_seq = geom
    assert rows_ctx % tm == 0 and rows_seq % tm == 0

    def index(i, *_):
        row0 = i * tm
        seq = jnp.where(row0 < rows_ctx, 0, 1 + (row0 - rows_ctx) // rows_seq)
        return (seq, which, 0, 0)
    return pl.BlockSpec((None, None, 1, d), index)


def _in_proj_kernel(x_ref, nw_ref, sc_ref, sh_ref, w_ref, o_ref, h_ref):
    @pl.when(pl.program_id(1) == 0)
    def _():
        h = _rms(x_ref[...], nw_ref[...]) * (1.0 + sc_ref[...]) + sh_ref[...]
        h_ref[...] = h.astype(BF16)

    o_ref[...] = _dot(h_ref[...], w_ref[...])


def _in_proj(x, nw, mod, w, geom):
    t, d = x.shape
    n = w.shape[1]
    tm, tn = TM_PROJ, 1024
    mspec = functools.partial(_mod_spec, d=d, tm=tm, geom=geom)
    return pl.pallas_call(
        _in_proj_kernel,
        grid=(t // tm, n // tn),
        in_specs=[pl.BlockSpec((tm, d), lambda i, j: (i, 0)),
                  pl.BlockSpec((1, d), lambda i, j: (0, 0)),
                  mspec(1), mspec(0),
                  pl.BlockSpec((d, tn), lambda i, j: (0, j))],
        out_specs=pl.BlockSpec((tm, tn), lambda i, j: (i, j)),
        out_shape=jax.ShapeDtypeStruct((t, n), F32),
        scratch_shapes=[pltpu.VMEM((tm, d), BF16)],
        compiler_params=_params("parallel", "arbitrary"),
        name="in_proj",
    )(x, nw.reshape(1, d), mod, mod, w)


def _cumsum_rows(tri, g):
    k = g.shape[1]
    g1 = g.astype(BF16)
    r1 = g - g1.astype(F32)
    g2 = r1.astype(BF16)
    g3 = (r1 - g2.astype(F32)).astype(BF16)
    bb = _dot(tri, jnp.concatenate([g1, g2, g3], axis=1))
    return (bb[:, 2 * k:] + bb[:, k:2 * k]) + bb[:, :k]


def _hgrn_chunk(q, k, v, g, st, reverse):
    c, kd = q.shape
    nb = c // HGRN_SUB
    rows = lax.broadcasted_iota(jnp.int32, (c, 1), 0)
    cols = lax.broadcasted_iota(jnp.int32, (1, c), 1)
    rows_sub = lax.broadcasted_iota(jnp.int32, (HGRN_SUB, 1), 0)
    tri = ((cols >= rows) if reverse else (cols <= rows)).astype(BF16)
    b = _cumsum_rows(tri, g)
    tot = b[0:1] if reverse else b[c - 1:c]
    o = _dot_nt((q * jnp.exp(b)).astype(BF16), st.astype(BF16))
    kdec = (k * jnp.exp(tot - b)).astype(BF16)
    st_new = st * jnp.exp(tot) + _dot(v.T.astype(BF16), kdec)
    vb = v.astype(BF16)
    o_blocks = []
    for blk in range(nb):
        lo, hi = blk * HGRN_SUB, (blk + 1) * HGRN_SUB
        b_i, q_i, k_i, g_i = b[lo:hi], q[lo:hi], k[lo:hi], g[lo:hi]
        if reverse:
            ref = b_i[HGRN_SUB - 1:HGRN_SUB] - g_i[HGRN_SUB - 1:HGRN_SUB]
            earlier = rows >= hi
            has_earlier = blk < nb - 1
        else:
            ref = b_i[0:1] - g_i[0:1]
            earlier = rows < lo
            has_earlier = blk > 0
        if has_earlier:
            q_s = (q_i * jnp.exp(b_i - ref)).astype(BF16)
            k_s = jnp.where(earlier, k * jnp.exp(jnp.minimum(ref - b, 0.0)), 0.0).astype(BF16)
            att = _dot_nt(q_s, k_s)
        else:
            att = jnp.zeros((HGRN_SUB, c), F32)
        for j in range(HGRN_SUB):
            keep = (rows_sub <= j) if reverse else (rows_sub >= j)
            e = jnp.exp(jnp.where(keep, b_i - b_i[j:j + 1], -jnp.inf))
            col = jnp.sum(q_i * k_i[j:j + 1] * e, axis=-1, keepdims=True)
            att = att + jnp.where(cols == lo + j, col, 0.0)
        o_blocks.append(_dot(att.astype(BF16), vb))
    return o + jnp.concatenate(o_blocks, axis=0), st_new


def _hgrn_kernel(*refs, seq_len, has_s0, want_final):
    q_ref, v_ref, xf_ref, xb_ref, low_ref = refs[:5]
    pos = 5
    s0_ref = None
    if has_s0:
        s0_ref = refs[pos]
        pos += 1
    o_ref = refs[pos]
    pos += 1
    sfin_ref = None
    if want_final:
        sfin_ref = refs[pos]
        pos += 1
    st_ref = refs[pos]

    c = HGRN_CHUNK
    n = seq_len // c
    for d in range(2):
        st_ref[d] = s0_ref[d].T if has_s0 else jnp.zeros(st_ref.shape[1:], F32)

    def direction(chunk, d, x_ref):
        rs = pl.ds(pl.multiple_of(chunk * c, c), c)
        low = low_ref[d]
        f = low + (1.0 - low) * jax.nn.sigmoid(x_ref[rs, :])
        o, st = _hgrn_chunk(_silu(q_ref[rs, :]), 1.0 - f, v_ref[rs, :], jnp.log(f),
                            st_ref[d], reverse=(d == 1))
        st_ref[d] = st
        return rs, o

    def first_half(i, carry):
        rs, o = direction(i, 0, xf_ref)
        o_ref[rs, :] = o
        rs, o = direction(n - 1 - i, 1, xb_ref)
        o_ref[rs, :] = o
        return carry

    def second_half(i, carry):
        rs, o = direction(i, 0, xf_ref)
        o_ref[rs, :] += o
        rs, o = direction(n - 1 - i, 1, xb_ref)
        o_ref[rs, :] += o
        return carry

    lax.fori_loop(0, n // 2, first_half, 0)
    lax.fori_loop(n // 2, n, second_half, 0)
    if want_final:
        for d in range(2):
            sfin_ref[d] = st_ref[d].T


def _hgrn(proj, lower, s0, row0, n_seq, seq_len, want_final):
    w_a = lower.shape[1]
    dk = w_a // HA_HEADS
    assert row0 % seq_len == 0 and seq_len % (2 * HGRN_CHUNK) == 0
    blk0 = row0 // seq_len
    ncol = w_a // dk

    def col(group):
        return pl.BlockSpec((seq_len, dk), lambda s, h: (blk0 + s, group * ncol + h))

    in_specs = [col(0), col(1), col(2), col(3),
                pl.BlockSpec((2, 1, dk), lambda s, h: (0, 0, h))]
    args = [proj, proj, proj, proj, lower.reshape(2, 1, w_a)]
    state_spec = pl.BlockSpec((None, 2, None, dk, dk), lambda s, h: (s, 0, h, 0, 0))
    if s0 is not None:
        in_specs.append(state_spec)
        args.append(s0)
    out_specs = [pl.BlockSpec((seq_len, dk), lambda s, h: (s, h))]
    out_shape = [jax.ShapeDtypeStruct((n_seq * seq_len, w_a), F32)]
    if want_final:
        out_specs.append(state_spec)
        out_shape.append(jax.ShapeDtypeStruct((n_seq, 2, HA_HEADS, dk, dk), F32))
    outs = pl.pallas_call(
        functools.partial(_hgrn_kernel, seq_len=seq_len, has_s0=s0 is not None,
                          want_final=want_final),
        grid=(n_seq, HA_HEADS),
        in_specs=in_specs,
        out_specs=out_specs,
        out_shape=out_shape,
        scratch_shapes=[pltpu.VMEM((2, dk, dk), F32)],
        compiler_params=_params("parallel", "parallel"),
        name="hgrn_scan_%d" % seq_len,
    )(*args)
    return outs[0], (outs[1] if want_final else None)


def _gmlp_kernel(u_ref, v_ref, nw_ref, ws_ref, bs_ref, o_ref):
    tm, w_b = u_ref.shape
    hd = w_b // HB_GROUPS
    for cc in range(tm // MLP_CHUNK):
        rs = slice(cc * MLP_CHUNK, (cc + 1) * MLP_CHUNK)
        vn = _rms(v_ref[rs, :], nw_ref[...]).astype(BF16)
        for g in range(HB_GROUPS):
            cs = slice(g * hd, (g + 1) * hd)
            s = _dot(ws_ref[g], vn[:, cs]) + bs_ref[g]
            o_ref[rs, cs] = (u_ref[rs, cs] * s).astype(o_ref.dtype)


def _gmlp(proj, nw, w_s, b_s, w_a, w_b):
    t = proj.shape[0]
    tm = TM_MIX
    ublk = 5 * w_a // w_b
    bs_b = jnp.broadcast_to(b_s[:, :, None], b_s.shape + (w_b // HB_GROUPS,))
    return pl.pallas_call(
        _gmlp_kernel,
        grid=(t // tm,),
        in_specs=[pl.BlockSpec((tm, w_b), lambda i: (i, ublk)),
                  pl.BlockSpec((tm, w_b), lambda i: (i, ublk + 1)),
                  pl.BlockSpec((1, w_b), lambda i: (0, 0)),
                  pl.BlockSpec(w_s.shape, lambda i: (0, 0, 0)),
                  pl.BlockSpec(bs_b.shape, lambda i: (0, 0, 0))],
        out_specs=pl.BlockSpec((tm, w_b), lambda i: (i, 0)),
        out_shape=jax.ShapeDtypeStruct((t, w_b), BF16),
        compiler_params=_params("parallel"),
        name="gmlp",
    )(proj, proj, nw.reshape(1, w_b), w_s, bs_b)


def _mix_out_kernel(oa_ref, hg_ref, ob_ref, onw_ref, w_ref, pnw_ref, gate_ref, x_ref, o_ref):
    w_a = oa_ref.shape[1]
    oa = (_rms(oa_ref[...], onw_ref[...]) * _silu(hg_ref[...])).astype(BF16)
    y = _dot(oa, w_ref[:w_a, :]) + _dot(ob_ref[...], w_ref[w_a:, :])
    o_ref[...] = x_ref[...] + gate_ref[...] * _rms(y, pnw_ref[...])


def _mix_out(o_a, proj, o_b, onw, w_out, pnw, mod, x, geom):
    t, d = x.shape
    w_a = o_a.shape[1]
    w_b = o_b.shape[1]
    assert w_a == w_b
    tm = TM_MIX
    mspec = functools.partial(_mod_spec, d=d, tm=tm, geom=geom)
    return pl.pallas_call(
        _mix_out_kernel,
        grid=(t // tm,),
        in_specs=[pl.BlockSpec((tm, w_a), lambda i: (i, 0)),
                  pl.BlockSpec((tm, w_a), lambda i: (i, 4)),
                  pl.BlockSpec((tm, w_b), lambda i: (i, 0)),
                  pl.BlockSpec((1, w_a), lambda i: (0, 0)),
                  pl.BlockSpec((w_a + w_b, d), lambda i: (0, 0)),
                  pl.BlockSpec((1, d), lambda i: (0, 0)),
                  mspec(2),
                  pl.BlockSpec((tm, d), lambda i: (i, 0))],
        out_specs=pl.BlockSpec((tm, d), lambda i: (i, 0)),
        out_shape=jax.ShapeDtypeStruct((t, d), F32),
        compiler_params=_params("parallel"),
        name="mix_out",
    )(o_a, proj, o_b, onw.reshape(1, w_a), w_out, pnw.reshape(1, d), mod, x)


def _ffn_kernel(x_ref, nw_ref, sc_ref, sh_ref, wg_ref, wu_ref, wd_ref, pnw_ref, gate_ref,
                o_ref, h_ref, acc_ref):
    j = pl.program_id(1)

    @pl.when(j == 0)
    def _():
        h = _rms(x_ref[...], nw_ref[...]) * (1.0 + sc_ref[...]) + sh_ref[...]
        h_ref[...] = h.astype(BF16)
        acc_ref[...] = jnp.zeros(acc_ref.shape, F32)

    h = h_ref[...]
    a = (_silu(_dot(h, wg_ref[...])) * _dot(h, wu_ref[...])).astype(BF16)
    acc_ref[...] += _dot(a, wd_ref[...])

    @pl.when(j == pl.num_programs(1) - 1)
    def _():
        o_ref[...] = x_ref[...] + gate_ref[...] * _rms(acc_ref[...], pnw_ref[...])


def _ffn(x, nw, mod, wg, wu, wd, pnw, geom):
    t, d = x.shape
    f = wg.shape[1]
    tm, tf = TM_FFN, TF_FFN
    mspec = functools.partial(_mod_spec, d=d, tm=tm, geom=geom)
    return pl.pallas_call(
        _ffn_kernel,
        grid=(t // tm, f // tf),
        in_specs=[pl.BlockSpec((tm, d), lambda i, j: (i, 0)),
                  pl.BlockSpec((1, d), lambda i, j: (0, 0)),
                  mspec(4), mspec(3),
                  pl.BlockSpec((d, tf), lambda i, j: (0, j)),
                  pl.BlockSpec((d, tf), lambda i, j: (0, j)),
                  pl.BlockSpec((tf, d), lambda i, j: (j, 0)),
                  pl.BlockSpec((1, d), lambda i, j: (0, 0)),
                  mspec(5)],
        out_specs=pl.BlockSpec((tm, d), lambda i, j: (i, 0)),
        out_shape=jax.ShapeDtypeStruct((t, d), F32),
        scratch_shapes=[pltpu.VMEM((tm, d), BF16), pltpu.VMEM((tm, d), F32)],
        compiler_params=_params("parallel", "arbitrary"),
        name="ffn_dense",
    )(x, nw.reshape(1, d), mod, mod, wg, wu, wd, pnw.reshape(1, d), mod)


def _router_kernel(x_ref, nw_ref, sc_ref, sh_ref, wr_ref, h_ref, wts_ref, idx_ref, cnt_ref,
                   run_ref, *, n_experts):
    i = pl.program_id(0)
    tm = x_ref.shape[0]
    lanes = wr_ref.shape[1]

    @pl.when(i == 0)
    def _():
        run_ref[...] = jnp.zeros(run_ref.shape, F32)

    h = _rms(x_ref[...], nw_ref[...]) * (1.0 + sc_ref[...]) + sh_ref[...]
    h_ref[...] = h
    lane = lax.broadcasted_iota(jnp.int32, (tm, lanes), 1)
    logits = jnp.dot(h, wr_ref[...], precision=lax.Precision.HIGHEST,
                     preferred_element_type=F32)
    logits = jnp.where(lane < n_experts, logits, -jnp.inf)
    m1 = jnp.max(logits, axis=-1, keepdims=True)
    i1 = jnp.min(jnp.where(logits == m1, lane, lanes), axis=-1, keepdims=True)
    rest = jnp.where(lane == i1, -jnp.inf, logits)
    m2 = jnp.max(rest, axis=-1, keepdims=True)
    i2 = jnp.min(jnp.where(rest == m2, lane, lanes), axis=-1, keepdims=True)
    e2 = jnp.exp(m2 - m1)
    w1 = 1.0 / (1.0 + e2)
    w2 = e2 / (1.0 + e2)
    hot1 = lane == i1
    hot2 = lane == i2
    taken = (hot1 | hot2).astype(F32)
    r_id = lax.broadcasted_iota(jnp.int32, (tm, tm), 0)
    c_id = lax.broadcasted_iota(jnp.int32, (tm, tm), 1)
    before = _dot((c_id < r_id).astype(BF16), taken.astype(BF16)) + run_ref[...]
    rank1 = jnp.sum(jnp.where(hot1, before, 0.0), axis=-1, keepdims=True)
    rank2 = jnp.sum(jnp.where(hot2, before, 0.0), axis=-1, keepdims=True)
    run_ref[...] += jnp.sum(taken, axis=0, keepdims=True)
    wts_ref[...] = jnp.where(lane == 0, w1, jnp.where(lane == 1, w2, 0.0))
    idx_ref[...] = jnp.where(
        lane == 0, i1, jnp.where(
            lane == 1, i2, jnp.where(
                lane == 2, rank1.astype(jnp.int32), jnp.where(
                    lane == 3, rank2.astype(jnp.int32), 0))))
    cnt_ref[...] = jnp.broadcast_to(run_ref[...], cnt_ref.shape).astype(jnp.int32)


def _router(x, nw, mod, w_router, geom):
    t, d = x.shape
    n_experts = w_router.shape[1]
    lanes = N_EXPERTS_LANES
    tm = TM_ROUTE
    wr = jnp.zeros((d, lanes), F32).at[:, :n_experts].set(w_router)
    mspec = functools.partial(_mod_spec, d=d, tm=tm, geom=geom)
    return pl.pallas_call(
        functools.partial(_router_kernel, n_experts=n_experts),
        grid=(t // tm,),
        in_specs=[pl.BlockSpec((tm, d), lambda i: (i, 0)),
                  pl.BlockSpec((1, d), lambda i: (0, 0)),
                  mspec(4), mspec(3),
                  pl.BlockSpec((d, lanes), lambda i: (0, 0))],
        out_specs=[pl.BlockSpec((tm, d), lambda i: (i, 0)),
                   pl.BlockSpec((tm, lanes), lambda i: (i, 0)),
                   pl.BlockSpec((tm, lanes), lambda i: (i, 0)),
                   pl.BlockSpec((8, lanes), lambda i: (0, 0))],
        out_shape=[jax.ShapeDtypeStruct((t, d), F32),
                   jax.ShapeDtypeStruct((t, lanes), F32),
                   jax.ShapeDtypeStruct((t, lanes), jnp.int32),
                   jax.ShapeDtypeStruct((8, lanes), jnp.int32)],
        scratch_shapes=[pltpu.VMEM((1, lanes), F32)],
        compiler_params=_params("arbitrary"),
        name="moe_router",
    )(x, nw.reshape(1, d), mod, mod, wr)


def _moe_kernel(te_ref, tv_ref, orow_ref, h_hbm, wg_ref, wu_ref, wd_ref, y_hbm,
                xf_ref, xb_ref, acc_ref, sem_ref, *, n_tokens):
    b = pl.program_id(0)
    j = pl.program_id(1)
    tm = xf_ref.shape[0]
    valid = tv_ref[b] > 0

    def row_copy(r, gather):
        orow = orow_ref[b, r]
        if gather:
            tok = jnp.where(orow >= n_tokens, orow - n_tokens, jnp.maximum(orow, 0))
            return None, pltpu.make_async_copy(h_hbm.at[pl.ds(tok, 1), :],
                                               xf_ref.at[pl.ds(r, 1), :], sem_ref.at[0])
        return orow >= 0, pltpu.make_async_copy(xf_ref.at[pl.ds(r, 1), :],
                                                y_hbm.at[pl.ds(jnp.maximum(orow, 0), 1), :],
                                                sem_ref.at[0])

    def row_copies(gather):
        def run(r, wait):
            real, cp = row_copy(r, gather)
            op = cp.wait if wait else cp.start
            if real is None:
                op()
            else:
                pl.when(real)(op)

        def start(r, carry):
            run(r, False)
            return carry

        def wait(r, carry):
            run(r, True)
            return carry
        lax.fori_loop(0, tm, start, 0)
        lax.fori_loop(0, tm, wait, 0)

    @pl.when(valid & (j == 0))
    def _():
        row_copies(True)
        xb_ref[...] = xf_ref[...].astype(BF16)
        acc_ref[...] = jnp.zeros(acc_ref.shape, F32)

    @pl.when(valid)
    def _():
        x = xb_ref[...]
        a = (_silu(_dot(x, wg_ref[...])) * _dot(x, wu_ref[...])).astype(BF16)
        acc_ref[...] += _dot(a, wd_ref[...])

    @pl.when(valid & (j == pl.num_programs(1) - 1))
    def _():
        xf_ref[...] = acc_ref[...]
        row_copies(False)


def _moe_experts(h, tile_expert, tile_valid, orow, wg, wu, wd):
    t, d = h.shape
    n_tiles, tm = orow.shape
    f = wg.shape[2]
    tf = TF_MOE
    grid_spec = pltpu.PrefetchScalarGridSpec(
        num_scalar_prefetch=3,
        grid=(n_tiles, f // tf),
        in_specs=[pl.BlockSpec(memory_space=pl.ANY),
                  pl.BlockSpec((None, d, tf), lambda b, j, te, tv, orow: (te[b], 0, j)),
                  pl.BlockSpec((None, d, tf), lambda b, j, te, tv, orow: (te[b], 0, j)),
                  pl.BlockSpec((None, tf, d), lambda b, j, te, tv, orow: (te[b], j, 0))],
        out_specs=pl.BlockSpec(memory_space=pl.ANY),
        scratch_shapes=[pltpu.VMEM((tm, d), F32), pltpu.VMEM((tm, d), BF16),
                        pltpu.VMEM((tm, d), F32), pltpu.SemaphoreType.DMA((1,))],
    )
    return pl.pallas_call(
        functools.partial(_moe_kernel, n_tokens=t),
        grid_spec=grid_spec,
        out_shape=jax.ShapeDtypeStruct((2 * t, d), F32),
        compiler_params=_params("arbitrary", "arbitrary"),
        name="moe_experts",
    )(tile_expert, tile_valid, orow, h, wg, wu, wd)


def _combine_kernel(y0_ref, y1_ref, wts_ref, pnw_ref, gate_ref, x_ref, o_ref):
    w = wts_ref[...]
    f = w[:, 0:1] * y0_ref[...] + w[:, 1:2] * y1_ref[...]
    o_ref[...] = x_ref[...] + gate_ref[...] * _rms(f, pnw_ref[...])


def _moe_combine(ybuf, wts, pnw, mod, x, geom):
    t, d = x.shape
    tm = TM_MIX
    nblk = t // tm
    mspec = functools.partial(_mod_spec, d=d, tm=tm, geom=geom)
    return pl.pallas_call(
        _combine_kernel,
        grid=(nblk,),
        in_specs=[pl.BlockSpec((tm, d), lambda i: (i, 0)),
                  pl.BlockSpec((tm, d), lambda i: (nblk + i, 0)),
                  pl.BlockSpec((tm, wts.shape[1]), lambda i: (i, 0)),
                  pl.BlockSpec((1, d), lambda i: (0, 0)),
                  mspec(5),
                  pl.BlockSpec((tm, d), lambda i: (i, 0))],
        out_specs=pl.BlockSpec((tm, d), lambda i: (i, 0)),
        out_shape=jax.ShapeDtypeStruct((t, d), F32),
        compiler_params=_params("parallel"),
        name="moe_combine",
    )(ybuf, ybuf, wts, pnw.reshape(1, d), mod, x)


def _moe(x, nw, mod, w_router, wg, wu, wd, pnw, geom):
    t, d = x.shape
    n_experts = w_router.shape[1]
    tm = TM_MOE
    h, wts, idx, cnt = _router(x, nw, mod, w_router, geom)
    counts = cnt[0, :n_experts]
    padded = (counts + tm - 1) // tm * tm
    ends = jnp.cumsum(padded)
    starts = ends - padded
    n_tiles = (2 * t + n_experts * (tm - 1)) // tm
    expert = idx[:, 0:2]
    dest = starts[expert] + idx[:, 2:4]
    out_row = jnp.arange(t, dtype=jnp.int32)[:, None] + jnp.array([0, t], jnp.int32)[None, :]
    orow = jnp.full((n_tiles * tm,), -1, jnp.int32)
    orow = orow.at[dest.reshape(-1)].set(out_row.reshape(-1)).reshape(n_tiles, tm)
    tile_start = jnp.arange(n_tiles, dtype=jnp.int32) * tm
    tile_expert = jnp.minimum(jnp.searchsorted(ends, tile_start, side="right"),
                              n_experts - 1).astype(jnp.int32)
    tile_valid = (tile_start < ends[-1]).astype(jnp.int32)
    ybuf = _moe_experts(h, tile_expert, tile_valid, orow, wg, wu, wd)
    return _moe_combine(ybuf, wts, pnw, mod, x, geom)


def kernel(x_prompt, x_sample, c, state_hgrn, c_ctx, w_mod, b_mod, pre_mix_norm, w_in,
           lb_logits, hgrn_out_norm, gmlp_v_norm, w_spatial, b_spatial, w_out, post_mix_norm,
           pre_ffn_norm, ffn_w_gate, ffn_w_up, ffn_w_down, post_ffn_norm, moe_router,
           moe_w_gate, moe_w_up, moe_w_down):
    batch, seq, d = x_prompt.shape
    dec_batch, dec_seq, _ = x_sample.shape
    depth = w_in.shape[0]
    w_a = lb_logits.shape[2]
    w_b = d - w_a
    rows_ctx = batch * seq
    geom = (rows_ctx, dec_seq)

    lb_soft = jax.nn.softmax(lb_logits.astype(F32), axis=0)
    lb_cum = jnp.cumsum(lb_soft, axis=0)
    lower = lb_cum - lb_cum[0:1]

    n_vec = 1 + dec_batch
    n_vec_pad = (n_vec + 7) // 8 * 8
    cvecs = jnp.zeros((n_vec_pad, d), F32).at[0].set(c_ctx).at[1:n_vec].set(c)
    mod_all = _mod_vectors(cvecs, w_mod, b_mod).reshape(depth, n_vec_pad, N_MOD, 1, d)

    x = jnp.concatenate([x_prompt.reshape(rows_ctx, d), x_sample.reshape(dec_batch * dec_seq, d)],
                        axis=0)
    ctx_states = []
    for l in range(depth):
        mod = mod_all[l]
        proj = _in_proj(x, pre_mix_norm[l], mod, w_in[l].astype(BF16), geom)
        o_ctx, s_fin = _hgrn(proj, lower[l], None, 0, batch, seq, True)
        o_lat, _ = _hgrn(proj, lower[l], state_hgrn[:, l], rows_ctx, dec_batch, dec_seq, False)
        o_a = jnp.concatenate([o_ctx, o_lat], axis=0)
        o_b = _gmlp(proj, gmlp_v_norm[l], w_spatial[l].astype(BF16), b_spatial[l], w_a, w_b)
        x = _mix_out(o_a, proj, o_b, hgrn_out_norm[l], w_out[l].astype(BF16), post_mix_norm[l],
                     mod, x, geom)
        i = l // 2
        if l % 2 == 0:
            x = _ffn(x, pre_ffn_norm[l], mod, ffn_w_gate[i].astype(BF16),
                     ffn_w_up[i].astype(BF16), ffn_w_down[i].astype(BF16), post_ffn_norm[l], geom)
        else:
            x = _moe(x, pre_ffn_norm[l], mod, moe_router[i], moe_w_gate[i].astype(BF16),
                     moe_w_up[i].astype(BF16), moe_w_down[i].astype(BF16), post_ffn_norm[l], geom)
        ctx_states.append(s_fin)
    new_state = jnp.stack(ctx_states, axis=1)
    y_prompt = x[:rows_ctx].reshape(batch, seq, d)
    y_sample = x[rows_ctx:].reshape(dec_batch, dec_seq, d)
    return (y_prompt, y_sample, new_state)
```
